```python
import math
import jax, jax.numpy as jnp
from jax import lax
import numpy as np

D_MODEL = 2048
BATCH = 4
SEQ = 2048
DEPTH = 2

SSM_WIDTH = D_MODEL // 2
SSM_GROUP = 16
SSM_GROUPS = SSM_WIDTH // SSM_GROUP
SSM_STATE = 64
DT_MIN = 0.001
DT_MAX = 0.1
ATTN_HEADS = 16
HEAD_DIM = 128
ATTN_WIDTH = ATTN_HEADS * HEAD_DIM
MOBA_BLOCK = 256
MOBA_TOPK = 3
Q_CHUNK = 16
ROT_DIM = HEAD_DIM // 4
ROPE_THETA = 500000.0
NORM_EPS = 1e-6
IN_SPLITS = (SSM_WIDTH, SSM_WIDTH, ATTN_WIDTH, ATTN_WIDTH, ATTN_WIDTH, ATTN_WIDTH, D_MODEL, D_MODEL)
IN_WIDTH = sum(IN_SPLITS)

kernel_name = "hybrid_s5_moba_gated_block"


def rms_norm(x, gain):
    xf = x.astype(jnp.float32)
    ms = jnp.mean(xf * xf, axis=-1, keepdims=True)
    return (xf * lax.rsqrt(ms + NORM_EPS) * gain.astype(jnp.float32)).astype(x.dtype)


def _complex_scan_op(e1, e2):
    a1r, a1i, b1r, b1i = e1
    a2r, a2i, b2r, b2i = e2
    return (a2r * a1r - a2i * a1i,
            a2r * a1i + a2i * a1r,
            a2r * b1r - a2i * b1i + b2r,
            a2r * b1i + a2i * b1r + b2i)


def s5_mixer(u, lam_re, lam_im, log_dt, b_re, b_im, c_re, c_im, d_skip):
    bsz, seq, _ = u.shape
    uf = u.astype(jnp.float32).reshape(bsz, seq, SSM_GROUPS, SSM_GROUP)
    lr = lam_re.astype(jnp.float32)
    li = lam_im.astype(jnp.float32)
    dt = jnp.exp(log_dt.astype(jnp.float32))[:, None]
    mag = jnp.exp(lr * dt)
    abar_re = mag * jnp.cos(li * dt)
    abar_im = mag * jnp.sin(li * dt)
    den = lr * lr + li * li
    nr = abar_re - 1.0
    ni = abar_im
    f_re = ((nr * lr + ni * li) / den)[..., None]
    f_im = ((ni * lr - nr * li) / den)[..., None]
    br = b_re.astype(jnp.float32)
    bi = b_im.astype(jnp.float32)
    bb_re = f_re * br - f_im * bi
    bb_im = f_re * bi + f_im * br
    bu_re = jnp.einsum('blgc,gpc->blgp', uf, bb_re)
    bu_im = jnp.einsum('blgc,gpc->blgp', uf, bb_im)
    full = bu_re.shape
    a_re = jnp.broadcast_to(abar_re, full)
    a_im = jnp.broadcast_to(abar_im, full)
    _, _, x_re, x_im = lax.associative_scan(_complex_scan_op, (a_re, a_im, bu_re, bu_im), axis=1)
    y = (jnp.einsum('blgp,gcp->blgc', x_re, c_re.astype(jnp.float32))
         - jnp.einsum('blgp,gcp->blgc', x_im, c_im.astype(jnp.float32))
         + d_skip.astype(jnp.float32).reshape(SSM_GROUPS, SSM_GROUP) * uf)
    return y.reshape(bsz, seq, SSM_WIDTH)


def partial_rope(t, cos, sin):
    half = ROT_DIM // 2
    t1 = t[..., :half]
    t2 = t[..., half:ROT_DIM]
    return jnp.concatenate([t1 * cos - t2 * sin, t2 * cos + t1 * sin, t[..., ROT_DIM:]], axis=-1)


def moba_attention(q, k, v):
    bsz, nh, seq, hd = q.shape
    nblk = -(-seq // MOBA_BLOCK)
    pad = nblk * MOBA_BLOCK - seq
    kp = jnp.pad(k, ((0, 0), (0, 0), (0, pad), (0, 0)))
    vp = jnp.pad(v, ((0, 0), (0, 0), (0, pad), (0, 0)))
    kb = kp.reshape(bsz, nh, nblk, MOBA_BLOCK, hd)
    vb = vp.reshape(bsz, nh, nblk, MOBA_BLOCK, hd)
    kmean = jnp.mean(kb.astype(jnp.float32), axis=3)
    ksel = min(MOBA_TOPK, nblk)
    n_chunks = seq // Q_CHUNK
    qc = q.reshape(bsz, nh, n_chunks, Q_CHUNK, hd).transpose(2, 0, 1, 3, 4)
    scale = 1.0 / math.sqrt(hd)
    b_ix = jnp.arange(bsz)[:, None, None, None]
    h_ix = jnp.arange(nh)[None, :, None, None]
    blk_ids = jnp.arange(nblk)

    def one_chunk(args):
        ci, qi = args
        q0 = ci * Q_CHUNK
        own = q0 // MOBA_BLOCK
        qpos = q0 + jnp.arange(Q_CHUNK)
        k_own = lax.dynamic_index_in_dim(kb, own, axis=2, keepdims=False)
        v_own = lax.dynamic_index_in_dim(vb, own, axis=2, keepdims=False)
        kpos = own * MOBA_BLOCK + jnp.arange(MOBA_BLOCK)
        s_own = jnp.einsum('bhqd,bhkd->bhqk', qi, k_own).astype(jnp.float32) * scale
        s_own = jnp.where(kpos[None, :] <= qpos[:, None], s_own, -jnp.inf)
        gate = jnp.einsum('bhqd,bhnd->bhqn', qi.astype(jnp.float32), kmean)
        gate = jnp.where(blk_ids < own, gate, -jnp.inf)
        _, idx = lax.top_k(gate, ksel)
        valid = idx < own
        k_sel = kb[b_ix, h_ix, idx]
        v_sel = vb[b_ix, h_ix, idx]
        s_sel = jnp.einsum('bhqd,bhqnkd->bhqnk', qi, k_sel).astype(jnp.float32) * scale
        s_sel = jnp.where(valid[..., None], s_sel, -jnp.inf)
        s_sel = s_sel.reshape(bsz, nh, Q_CHUNK, ksel * MOBA_BLOCK)
        p = jax.nn.softmax(jnp.concatenate([s_own, s_sel], axis=-1), axis=-1)
        p_own = p[..., :MOBA_BLOCK].astype(v.dtype)
        p_sel = p[..., MOBA_BLOCK:].reshape(bsz, nh, Q_CHUNK, ksel, MOBA_BLOCK).astype(v.dtype)
        o = (jnp.einsum('bhqk,bhkd->bhqd', p_own, v_own)
             + jnp.einsum('bhqnk,bhqnkd->bhqd', p_sel, v_sel))
        return o.astype(q.dtype)

    out = lax.map(one_chunk, (jnp.arange(n_chunks), qc))
    return out.transpose(1, 2, 0, 3, 4).reshape(bsz, nh, seq, hd)


def setup_inputs(seed: int = 0) -> dict:
    key = jax.random.key(seed)
    ks = jax.random.split(key, 20)
    f32 = jnp.float32
    L, G, P, GC = DEPTH, SSM_GROUPS, SSM_STATE, SSM_GROUP
    x = jax.random.normal(ks[0], (BATCH, SEQ, D_MODEL), f32)
    pre_norm = 1.0 + 0.02 * jax.random.normal(ks[1], (L, D_MODEL), f32)
    post_norm = 1.0 + 0.02 * jax.random.normal(ks[2], (L, D_MODEL), f32)
    w_in = jax.random.normal(ks[3], (L, D_MODEL, IN_WIDTH), f32) * D_MODEL ** -0.5
    n = jnp.arange(P, dtype=f32)
    lam_re = -0.5 + 0.01 * jax.random.normal(ks[4], (L, G, P), f32)
    lam_im = jnp.pi * n[None, None, :] + 0.01 * jax.random.normal(ks[5], (L, G, P), f32)
    log_dt = jax.random.uniform(ks[6], (L, G), f32, math.log(DT_MIN), math.log(DT_MAX))
    b_re = jax.random.normal(ks[7], (L, G, P, GC), f32) * (2.0 * GC) ** -0.5
    b_im = jax.random.normal(ks[8], (L, G, P, GC), f32) * (2.0 * GC) ** -0.5
    c_re = jax.random.normal(ks[9], (L, G, GC, P), f32) * (2.0 * P) ** -0.5
    c_im = jax.random.normal(ks[10], (L, G, GC, P), f32) * (2.0 * P) ** -0.5
    d_skip = jax.random.normal(ks[11], (L, SSM_WIDTH), f32) * 0.5
    w_glu = jax.random.normal(ks[12], (L, SSM_WIDTH, SSM_WIDTH), f32) * SSM_WIDTH ** -0.5
    b_glu = 0.01 * jax.random.normal(ks[13], (L, SSM_WIDTH), f32)
    w_br_ssm = jax.random.normal(ks[14], (L, SSM_WIDTH, D_MODEL), f32) * SSM_WIDTH ** -0.5
    w_br_attn = jax.random.normal(ks[15], (L, ATTN_WIDTH, D_MODEL), f32) * ATTN_WIDTH ** -0.5
    w_out = jax.random.normal(ks[16], (L, D_MODEL, D_MODEL), f32) * D_MODEL ** -0.5
    return {"x": x, "pre_norm": pre_norm, "post_norm": post_norm, "w_in": w_in,
            "lam_re": lam_re, "lam_im": lam_im, "log_dt": log_dt,
            "b_re": b_re, "b_im": b_im, "c_re": c_re, "c_im": c_im, "d_skip": d_skip,
            "w_glu": w_glu, "b_glu": b_glu, "w_br_ssm": w_br_ssm, "w_br_attn": w_br_attn,
            "w_out": w_out}


def reference(x, pre_norm, post_norm, w_in, lam_re, lam_im, log_dt, b_re, b_im, c_re, c_im,
              d_skip, w_glu, b_glu, w_br_ssm, w_br_attn, w_out):
    bsz, seq, _ = x.shape
    pos = jnp.arange(seq, dtype=jnp.float32)
    inv_freq = ROPE_THETA ** (-jnp.arange(0, ROT_DIM, 2, dtype=jnp.float32) / ROT_DIM)
    ang = pos[:, None] * inv_freq[None, :]
    cos = jnp.cos(ang).astype(x.dtype)
    sin = jnp.sin(ang).astype(x.dtype)
    split_at = [int(s) for s in np.cumsum(IN_SPLITS)[:-1]]

    for l in range(DEPTH):
        h = rms_norm(x, pre_norm[l])
        proj = h @ w_in[l]
        u_s, z_s, q, k, v, z_a, g_s, g_a = jnp.split(proj, split_at, axis=-1)

        y_s = s5_mixer(u_s, lam_re[l], lam_im[l], log_dt[l], b_re[l], b_im[l],
                       c_re[l], c_im[l], d_skip[l]).astype(x.dtype)
        y_s = jax.nn.gelu(y_s)
        y_s = y_s * jax.nn.sigmoid(y_s @ w_glu[l] + b_glu[l])
        o_s = y_s * jax.nn.silu(z_s)

        def heads(t):
            return t.reshape(bsz, seq, ATTN_HEADS, HEAD_DIM).transpose(0, 2, 1, 3)
        qh = partial_rope(heads(q), cos, sin)
        kh = partial_rope(heads(k), cos, sin)
        vh = heads(v)
        o_a = moba_attention(qh, kh, vh).transpose(0, 2, 1, 3).reshape(bsz, seq, ATTN_WIDTH)
        o_a = o_a * jax.nn.silu(z_a)

        merged = (jax.nn.sigmoid(g_s) * (o_s @ w_br_ssm[l])
                  + jax.nn.sigmoid(g_a) * (o_a @ w_br_attn[l]))
        out = merged @ w_out[l]
        x = x + rms_norm(out, post_norm[l])
    return x
```

```python
import functools
import math

import jax
import jax.numpy as jnp
from jax import lax
from jax.experimental import pallas as pl
from jax.experimental.pallas import tpu as pltpu

F32 = jnp.float32
BF16 = jnp.bfloat16

D_MODEL = 2048
SSM_WIDTH = 1024
SSM_GROUP = 16
SSM_GROUPS = 64
SSM_STATE = 64
ATTN_HEADS = 16
HEAD_DIM = 128
ATTN_WIDTH = 2048
MOBA_BLOCK = 256
MOBA_TOPK = 3
ROT_DIM = 32
ROPE_THETA = 500000.0
NORM_EPS = 1e-6
IN_WIDTH = 14336
COL_U, COL_ZS, COL_Q, COL_K, COL_V, COL_ZA, COL_GS, COL_GA = 0, 1024, 2048, 4096, 6144, 8192, 10240, 12288

S5_CHUNK = 16
S5_PAIR_LANES = 2 * SSM_STATE

V7X_VMEM_BYTES = 64 * 1024 * 1024
NEG_BIG = -1e30


def _params(sem, vmem_bytes):
    limit = min(int(vmem_bytes * 1.25) + (4 << 20), V7X_VMEM_BYTES - (6 << 20))
    return pltpu.CompilerParams(dimension_semantics=sem, vmem_limit_bytes=limit)


def _inproj_kernel(x_ref, g_ref, w_ref, o_ref, h_ref):
    @pl.when(pl.program_id(1) == 0)
    def _():
        x = x_ref[...]
        ms = jnp.mean(x * x, axis=-1, keepdims=True)
        h_ref[...] = (x * lax.rsqrt(ms + NORM_EPS) * g_ref[...]).astype(BF16)

    o_ref[...] = jnp.dot(h_ref[...], w_ref[...], preferred_element_type=F32).astype(o_ref.dtype)


def _inproj(x2, gain, w_bf16, tm=1024, tn=1024):
    t, d = x2.shape
    n = w_bf16.shape[1]
    vmem = 2 * tm * d * 4 + tm * d * 2 + 2 * d * tn * 2 + 2 * tm * tn * 2 + tm * tn * 4
    return pl.pallas_call(
        _inproj_kernel,
        out_shape=jax.ShapeDtypeStruct((t, n), BF16),
        grid=(t // tm, n // tn),
        in_specs=[pl.BlockSpec((tm, d), lambda i, j: (i, 0)),
                  pl.BlockSpec((1, d), lambda i, j: (0, 0)),
                  pl.BlockSpec((d, tn), lambda i, j: (0, j))],
        out_specs=pl.BlockSpec((tm, tn), lambda i, j: (i, j)),
        scratch_shapes=[pltpu.VMEM((tm, d), BF16)],
        compiler_params=_params(("arbitrary", "arbitrary"), vmem),
        name="inproj",
    )(x2, gain.reshape(1, d), w_bf16)


def _s5_weights(lam_re, lam_im, log_dt, b_re, b_im, c_re, c_im):
    hi = lax.Precision.HIGHEST
    g, p = lam_re.shape
    c = SSM_GROUP
    n = S5_CHUNK
    lr = lam_re.astype(F32)
    li = lam_im.astype(F32)
    dt = jnp.exp(log_dt.astype(F32))[:, None]
    j = jnp.arange(n + 1, dtype=F32)[:, None, None]
    mag = jnp.exp(j * (lr * dt)[None])
    pr = mag * jnp.cos(j * (li * dt)[None])
    pi = mag * jnp.sin(j * (li * dt)[None])
    ar, ai = pr[1], pi[1]
    den = lr * lr + li * li
    nr = ar - 1.0
    ni = ai
    f_re = ((nr * lr + ni * li) / den)[..., None]
    f_im = ((ni * lr - nr * li) / den)[..., None]
    br = b_re.astype(F32)
    bi = b_im.astype(F32)
    bbr = f_re * br - f_im * bi
    bbi = f_re * bi + f_im * br
    er = pr[:n, :, :, None] * bbr[None] - pi[:n, :, :, None] * bbi[None]
    ei = pr[:n, :, :, None] * bbi[None] + pi[:n, :, :, None] * bbr[None]
    cr = c_re.astype(F32)
    ci = c_im.astype(F32)
    kern = (jnp.einsum('gop,tgpc->tgoc', cr, er, precision=hi)
            - jnp.einsum('gop,tgpc->tgoc', ci, ei, precision=hi))
    tt = jnp.arange(n)
    lag = tt[None, :] - tt[:, None]
    kg = jnp.where((lag >= 0)[:, :, None, None, None], kern[jnp.clip(lag, 0, n - 1)], 0.0)
    toeplitz = kg.transpose(2, 0, 4, 1, 3).reshape(g, n * c, n * c)

    even = (jnp.arange(g) % 2 == 0)[:, None, None]
    zero64 = jnp.zeros((g, n * c, p), F32)
    s_re = er[::-1].transpose(1, 0, 3, 2).reshape(g, n * c, p)
    s_im = ei[::-1].transpose(1, 0, 3, 2).reshape(g, n * c, p)
    to_state = jnp.concatenate([jnp.where(even, s_re, zero64), jnp.where(even, zero64, s_re),
                                jnp.where(even, s_im, zero64), jnp.where(even, zero64, s_im)], axis=-1)

    pr1 = pr[1:, :, None, :]
    pi1 = pi[1:, :, None, :]
    wyr = (cr[None] * pr1 - ci[None] * pi1).transpose(1, 3, 0, 2).reshape(g, p, n * c)
    wyi = (-(cr[None] * pi1 + ci[None] * pr1)).transpose(1, 3, 0, 2).reshape(g, p, n * c)
    zrow = jnp.zeros((g, p, n * c), F32)
    from_state = jnp.concatenate([jnp.where(even, wyr, zrow), jnp.where(even, zrow, wyr),
                                  jnp.where(even, wyi, zrow), jnp.where(even, zrow, wyi)], axis=1)
    a16_re = pr[n].reshape(1, g * p)
    a16_im = pi[n].reshape(1, g * p)
    return toeplitz.astype(BF16), to_state.astype(BF16), from_state.astype(BF16), a16_re, a16_im


def _s5_state_kernel(u_ref, w_ref, sr_ref, si_ref):
    r = (jnp.dot(u_ref[0], w_ref[0], preferred_element_type=F32)
         + jnp.dot(u_ref[1], w_ref[1], preferred_element_type=F32))
    sr_ref[...] = r[:, :S5_PAIR_LANES]
    si_ref[...] = r[:, S5_PAIR_LANES:]


def _s5_scan_kernel(sr_ref, si_ref, ar_ref, ai_ref, xr_ref, xi_ref, *, nbatch, nchunks):
    ar = ar_ref[...]
    ai = ai_ref[...]
    lanes = sr_ref.shape[1]
    per = 8 // nbatch

    def body(j, carry):
        xr, xi = carry
        r0 = pl.multiple_of(j * 8, 8)
        sr = sr_ref[pl.ds(r0, 8), :]
        si = si_ref[pl.ds(r0, 8), :]
        outs_r, outs_i = [], []
        for q in range(per):
            outs_r.append(xr)
            outs_i.append(xi)
            s_r = sr[q * nbatch:(q + 1) * nbatch]
            s_i = si[q * nbatch:(q + 1) * nbatch]
            xr, xi = ar * xr - ai * xi + s_r, ar * xi + ai * xr + s_i
        xr_ref[pl.ds(r0, 8), :] = jnp.concatenate(outs_r, axis=0)
        xi_ref[pl.ds(r0, 8), :] = jnp.concatenate(outs_i, axis=0)
        return xr, xi

    zero = jnp.zeros((nbatch, lanes), F32)
    lax.fori_loop(0, nchunks // per, body, (zero, zero))


def _s5_out_kernel(u_ref, m_ref, xr_ref, xi_ref, w_ref, y_ref):
    x = jnp.concatenate([xr_ref[...], xi_ref[...]], axis=1).astype(BF16)
    for q in range(2):
        y_ref[q] = (jnp.dot(u_ref[q], m_ref[q], preferred_element_type=F32)
                    + jnp.dot(x, w_ref[q], preferred_element_type=F32))


def _s5_mixer(u_t, weights, nbatch, nchunks):
    toeplitz, to_state, from_state, a16_re, a16_im = weights
    g, rows, k = u_t.shape
    pairs = g // 2
    state_lanes = pairs * S5_PAIR_LANES
    blk3 = lambda j: (j, 0, 0)
    s_re, s_im = pl.pallas_call(
        _s5_state_kernel,
        out_shape=[jax.ShapeDtypeStruct((rows, state_lanes), F32)] * 2,
        grid=(pairs,),
        in_specs=[pl.BlockSpec((2, rows, k), blk3), pl.BlockSpec((2, k, k), blk3)],
        out_specs=[pl.BlockSpec((rows, S5_PAIR_LANES), lambda j: (0, j))] * 2,
        compiler_params=_params(("arbitrary",), 4 * rows * k * 2 + 4 * k * k * 2 + 6 * rows * k * 4),
        name="s5_state",
    )(u_t, to_state)

    lane_blk = 1024
    x_re, x_im = pl.pallas_call(
        functools.partial(_s5_scan_kernel, nbatch=nbatch, nchunks=nchunks),
        out_shape=[jax.ShapeDtypeStruct((rows, state_lanes), F32)] * 2,
        grid=(state_lanes // lane_blk,),
        in_specs=[pl.BlockSpec((rows, lane_blk), lambda j: (0, j))] * 2
                 + [pl.BlockSpec((1, lane_blk), lambda j: (0, j))] * 2,
        out_specs=[pl.BlockSpec((rows, lane_blk), lambda j: (0, j))] * 2,
        compiler_params=_params(("arbitrary",), 8 * rows * lane_blk * 4),
        name="s5_scan",
    )(s_re, s_im, a16_re, a16_im)

    return pl.pallas_call(
        _s5_out_kernel,
        out_shape=jax.ShapeDtypeStruct((g, rows, k), F32),
        grid=(pairs,),
        in_specs=[pl.BlockSpec((2, rows, k), blk3), pl.BlockSpec((2, k, k), blk3),
                  pl.BlockSpec((rows, S5_PAIR_LANES), lambda j: (0, j)),
                  pl.BlockSpec((rows, S5_PAIR_LANES), lambda j: (0, j)),
                  pl.BlockSpec((2, k, k), blk3)],
        out_specs=pl.BlockSpec((2, rows, k), blk3),
        compiler_params=_params(("arbitrary",), 4 * rows * k * 2 + 8 * k * k * 2 + 8 * rows * k * 4),
        name="s5_out",
    )(u_t, toeplitz, x_re, x_im, from_state)


def _glu_kernel(y_ref, u_ref, z_ref, d_ref, w_ref, b_ref, o_ref):
    y = y_ref[...] + d_ref[...] * u_ref[...].astype(F32)
    y = jax.nn.gelu(y, approximate=True)
    gate = jnp.dot(y.astype(BF16), w_ref[...], preferred_element_type=F32) + b_ref[...]
    z = z_ref[...].astype(F32)
    o_ref[...] = (y * jax.nn.sigmoid(gate) * (z * jax.nn.sigmoid(z))).astype(o_ref.dtype)


def _glu(y, proj, d_skip, w_glu_bf16, b_glu, tm=1024):
    t, w = y.shape
    vmem = 2 * tm * w * 4 + 4 * tm * w * 2 + 2 * w * w * 2 + 2 * tm * w * 2 + 4 * tm * w * 4
    return pl.pallas_call(
        _glu_kernel,
        out_shape=jax.ShapeDtypeStruct((t, w), BF16),
        grid=(t // tm,),
        in_specs=[pl.BlockSpec((tm, w), lambda i: (i, 0)),
                  pl.BlockSpec((tm, w), lambda i: (i, COL_U // w)),
                  pl.BlockSpec((tm, w), lambda i: (i, COL_ZS // w)),
                  pl.BlockSpec((1, w), lambda i: (0, 0)),
                  pl.BlockSpec((w, w), lambda i: (0, 0)),
                  pl.BlockSpec((1, w), lambda i: (0, 0))],
        out_specs=pl.BlockSpec((tm, w), lambda i: (i, 0)),
        compiler_params=_params(("arbitrary",), vmem),
        name="glu",
    )(y, proj, proj, d_skip.reshape(1, w), w_glu_bf16, b_glu.reshape(1, w))


def _rope_tables(seq):
    half = ROT_DIM // 2
    pos = jnp.arange(seq, dtype=F32)
    inv_freq = ROPE_THETA ** (-jnp.arange(0, ROT_DIM, 2, dtype=F32) / ROT_DIM)
    ang = pos[:, None] * inv_freq[None, :]
    cos = jnp.cos(ang)
    sin = jnp.sin(ang)
    rest = HEAD_DIM - ROT_DIM
    c = jnp.concatenate([cos, cos, jnp.ones((seq, rest), F32)], axis=1)
    s_lo = jnp.concatenate([-sin, jnp.zeros((seq, HEAD_DIM - half), F32)], axis=1)
    s_hi = jnp.concatenate([jnp.zeros((seq, half), F32), sin, jnp.zeros((seq, rest), F32)], axis=1)
    return c, s_lo, s_hi


def _rope(t, c, s_lo, s_hi):
    half = ROT_DIM // 2
    return t * c + pltpu.roll(t, HEAD_DIM - half, 1) * s_lo + pltpu.roll(t, half, 1) * s_hi


def _attn_kernel(q_ref, k_ref, v_ref, z_ref, c_ref, slo_ref, shi_ref, o_ref, krot_ref, kmt_ref):
    qi = pl.program_id(2)
    blk = MOBA_BLOCK
    nblk = k_ref.shape[0] // blk
    scale = 1.0 / math.sqrt(HEAD_DIM)

    @pl.when(qi == 0)
    def _():
        kr = _rope(k_ref[...].astype(F32), c_ref[...], slo_ref[...], shi_ref[...])
        krot_ref[...] = kr.astype(BF16)
        means = [jnp.sum(kr[n * blk:(n + 1) * blk], axis=0, keepdims=True) * (1.0 / blk) for n in range(nblk)]
        km = jnp.concatenate(means + [jnp.zeros((HEAD_DIM - nblk, HEAD_DIM), F32)], axis=0)
        kmt_ref[...] = km.T

    r0 = pl.multiple_of(qi * blk, blk)
    rows = pl.ds(r0, blk)
    qr = _rope(q_ref[...].astype(F32), c_ref[rows, :], slo_ref[rows, :], shi_ref[rows, :])

    gate = jnp.dot(qr, kmt_ref[...], preferred_element_type=F32, precision=lax.Precision.HIGHEST)
    lane = lax.broadcasted_iota(jnp.int32, gate.shape, 1)
    rank = jnp.zeros(gate.shape, F32)
    for n in range(nblk - 1):
        col = gate[:, n:n + 1]
        beats = (col > gate) | ((col == gate) & (n < lane))
        rank = rank + jnp.where(beats, 1.0, 0.0) * jnp.where(n < qi, 1.0, 0.0)
    sel = jnp.where((lane < qi) & (rank < MOBA_TOPK), 1.0, 0.0)

    qb = qr.astype(BF16)
    nt = (((1,), (1,)), ((), ()))

    s = lax.dot_general(qb, krot_ref[rows, :], nt, preferred_element_type=F32) * scale
    row_id = lax.broadcasted_iota(jnp.int32, s.shape, 0)
    col_id = lax.broadcasted_iota(jnp.int32, s.shape, 1)
    s = jnp.where(col_id <= row_id, s, NEG_BIG)
    m0 = jnp.max(s, axis=-1, keepdims=True)
    p = jnp.exp(s - m0)
    l0 = jnp.sum(p, axis=-1, keepdims=True)
    acc0 = jnp.dot(p.astype(BF16), v_ref[rows, :], preferred_element_type=F32)

    def body(n, carry):
        m, l, acc = carry
        kv_rows = pl.ds(pl.multiple_of(n * blk, blk), blk)
        s = lax.dot_general(qb, krot_ref[kv_rows, :], nt, preferred_element_type=F32) * scale
        chosen = jnp.sum(jnp.where(lane == n, sel, 0.0), axis=-1, keepdims=True)
        s = jnp.where(chosen > 0.5, s, NEG_BIG)
        m_new = jnp.maximum(m, jnp.max(s, axis=-1, keepdims=True))
        alpha = jnp.exp(m - m_new)
        p = jnp.exp(s - m_new)
        l = alpha * l + jnp.sum(p, axis=-1, keepdims=True)
        acc = alpha * acc + jnp.dot(p.astype(BF16), v_ref[kv_rows, :], preferred_element_type=F32)
        return m_new, l, acc

    _, l, acc = lax.fori_loop(0, qi, body, (m0, l0, acc0))
    z = z_ref[...].astype(F32)
    o_ref[...] = ((acc / l) * (z * jax.nn.sigmoid(z))).astype(o_ref.dtype)


def _attention(proj, bsz, seq, tables):
    t = proj.shape[0]
    blk = MOBA_BLOCK
    nq = seq // blk
    hd = HEAD_DIM
    c, s_lo, s_hi = tables
    full = lambda b, h, i: (0, 0)
    vmem = 4 * seq * hd * 2 + 6 * seq * hd * 4 + seq * hd * 2 + 8 * blk * hd * 4 + 12 * blk * blk * 4 + 4 * seq * hd * 4
    return pl.pallas_call(
        _attn_kernel,
        out_shape=jax.ShapeDtypeStruct((t, ATTN_WIDTH), BF16),
        grid=(bsz, ATTN_HEADS, nq),
        in_specs=[pl.BlockSpec((blk, hd), lambda b, h, i: (b * nq + i, COL_Q // hd + h)),
                  pl.BlockSpec((seq, hd), lambda b, h, i: (b, COL_K // hd + h)),
                  pl.BlockSpec((seq, hd), lambda b, h, i: (b, COL_V // hd + h)),
                  pl.BlockSpec((blk, hd), lambda b, h, i: (b * nq + i, COL_ZA // hd + h)),
                  pl.BlockSpec((seq, hd), full), pl.BlockSpec((seq, hd), full), pl.BlockSpec((seq, hd), full)],
        out_specs=pl.BlockSpec((blk, hd), lambda b, h, i: (b * nq + i, h)),
        scratch_shapes=[pltpu.VMEM((seq, hd), BF16), pltpu.VMEM((hd, hd), F32)],
        compiler_params=_params(("arbitrary", "arbitrary", "arbitrary"), vmem),
        name="moba_attn",
    )(proj, proj, proj, proj, c, s_lo, s_hi)


def _merge_kernel(os_ref, oa_ref, ws_ref, wa_ref, gs_ref, ga_ref, o_ref):
    a = jnp.dot(os_ref[...], ws_ref[...], preferred_element_type=F32)
    b = jnp.dot(oa_ref[...], wa_ref[...], preferred_element_type=F32)
    gs = jax.nn.sigmoid(gs_ref[...].astype(F32))
    ga = jax.nn.sigmoid(ga_ref[...].astype(F32))
    o_ref[...] = (gs * a + ga * b).astype(o_ref.dtype)


def _merge(o_s, o_a, proj, w_s_bf16, w_a_bf16, tm=1024, tn=1024):
    t = o_s.shape[0]
    ks, ka = o_s.shape[1], o_a.shape[1]
    n = w_s_bf16.shape[1]
    vmem = 2 * (tm * ks + tm * ka + ks * tn + ka * tn + 3 * tm * tn) * 2 + 4 * tm * tn * 4
    return pl.pallas_call(
        _merge_kernel,
        out_shape=jax.ShapeDtypeStruct((t, n), BF16),
        grid=(t // tm, n // tn),
        in_specs=[pl.BlockSpec((tm, ks), lambda i, j: (i, 0)),
                  pl.BlockSpec((tm, ka), lambda i, j: (i, 0)),
                  pl.BlockSpec((ks, tn), lambda i, j: (0, j)),
                  pl.BlockSpec((ka, tn), lambda i, j: (0, j)),
                  pl.BlockSpec((tm, tn), lambda i, j: (i, COL_GS // tn + j)),
                  pl.BlockSpec((tm, tn), lambda i, j: (i, COL_GA // tn + j))],
        out_specs=pl.BlockSpec((tm, tn), lambda i, j: (i, j)),
        compiler_params=_params(("arbitrary", "arbitrary"), vmem),
        name="merge",
    )(o_s, o_a, w_s_bf16, w_a_bf16, proj, proj)


def _out_kernel(m_ref, w_ref, x_ref, g_ref, o_ref):
    out = jnp.dot(m_ref[...], w_ref[...], preferred_element_type=F32)
    ms = jnp.mean(out * out, axis=-1, keepdims=True)
    o_ref[...] = x_ref[...] + out * lax.rsqrt(ms + NORM_EPS) * g_ref[...]


def _outproj(merged, w_bf16, x2, gain, tm=512):
    t, d = x2.shape
    vmem = 2 * tm * d * 2 + 2 * d * d * 2 + 4 * tm * d * 4 + 2 * tm * d * 4
    return pl.pallas_call(
        _out_kernel,
        out_shape=jax.ShapeDtypeStruct((t, d), F32),
        grid=(t // tm,),
        in_specs=[pl.BlockSpec((tm, d), lambda i: (i, 0)),
                  pl.BlockSpec((d, d), lambda i: (0, 0)),
                  pl.BlockSpec((tm, d), lambda i: (i, 0)),
                  pl.BlockSpec((1, d), lambda i: (0, 0))],
        out_specs=pl.BlockSpec((tm, d), lambda i: (i, 0)),
        compiler_params=_params(("arbitrary",), vmem),
        name="outproj",
    )(merged, w_bf16, x2, gain.reshape(1, d))


def kernel(x, pre_norm, post_norm, w_in, lam_re, lam_im, log_dt, b_re, b_im, c_re, c_im, d_skip, w_glu, b_glu, w_br_ssm, w_br_attn, w_out):
    bsz, seq, d = x.shape
    depth = w_in.shape[0]
    t = bsz * seq
    nchunks = seq // S5_CHUNK
    g, c = SSM_GROUPS, SSM_GROUP
    tables = _rope_tables(seq)
    x2 = x.reshape(t, d)
    for l in range(depth):
        proj = _inproj(x2, pre_norm[l], w_in[l].astype(BF16))

        weights = _s5_weights(lam_re[l], lam_im[l], log_dt[l], b_re[l], b_im[l], c_re[l], c_im[l])
        u = proj[:, COL_U:COL_U + SSM_WIDTH]
        u_t = u.reshape(bsz, nchunks, S5_CHUNK, g, c).transpose(3, 1, 0, 2, 4).reshape(g, nchunks * bsz, S5_CHUNK * c)
        y_t = _s5_mixer(u_t, weights, bsz, nchunks)
        y = y_t.reshape(g, nchunks, bsz, S5_CHUNK, c).transpose(2, 1, 3, 0, 4).reshape(t, SSM_WIDTH)
        o_s = _glu(y, proj, d_skip[l], w_glu[l].astype(BF16), b_glu[l])

        o_a = _attention(proj, bsz, seq, tables)

        merged = _merge(o_s, o_a, proj, w_br_ssm[l].astype(BF16), w_br_attn[l].astype(BF16))
        x2 = _outproj(merged, w_out[l].astype(BF16), x2, post_norm[l])
    return x2.reshape(bsz, seq, d)
```
